```python
import jax, jax.numpy as jnp
from jax import lax
import numpy as np

D_MODEL = 2048
BATCH = 4
SEQ = 8192
DEPTH = 2

GRID_W = 64
PLE_DIM = 256
EPS = 1e-6
NA_HEAD_DIM = 128
NA_HEADS = D_MODEL // NA_HEAD_DIM
NA_WIDTH = NA_HEADS * NA_HEAD_DIM
NA_KH_MAX = 8
NA_KW = 16
RET_HEADS = 8
RET_QK_DIM = D_MODEL // RET_HEADS
RET_V_DIM = D_MODEL // RET_HEADS
RET_QK_WIDTH = RET_HEADS * RET_QK_DIM
RET_V_WIDTH = RET_HEADS * RET_V_DIM
RET_CHUNK = 128
ROPE_BASE = 10000.0
IN_SPLITS = (NA_WIDTH, NA_WIDTH, NA_WIDTH, RET_QK_WIDTH, RET_QK_WIDTH, RET_V_WIDTH, RET_V_WIDTH, D_MODEL, D_MODEL)
N_IN = 3 * NA_WIDTH + 2 * RET_QK_WIDTH + 2 * RET_V_WIDTH + 2 * D_MODEL
N_GROUPS = 4
EXPERTS_PER_GROUP = 8
N_EXPERTS = N_GROUPS * EXPERTS_PER_GROUP
TOP_K = 2
D_FF_EXPERT = D_MODEL // 4
MOE_BLOCK = 128

kernel_name = "hybrid_na_retention_hmoe_encoder"


def rmsnorm(x, g):
    xf = x.astype(jnp.float32)
    y = xf * lax.rsqrt(jnp.mean(xf * xf, axis=-1, keepdims=True) + EPS)
    return (y * g.astype(jnp.float32)).astype(x.dtype)


def rotary(x, pos):
    half = x.shape[-1] // 2
    inv = ROPE_BASE ** (-jnp.arange(half, dtype=jnp.float32) / half)
    ang = pos.astype(jnp.float32)[:, None] * inv[None, :]
    cos = jnp.cos(ang)[None, :, None, :].astype(x.dtype)
    sin = jnp.sin(ang)[None, :, None, :].astype(x.dtype)
    x1, x2 = x[..., :half], x[..., half:]
    return jnp.concatenate([x1 * cos - x2 * sin, x1 * sin + x2 * cos], axis=-1)


def neighbourhood_attention(q, k, v, rpb):
    B, S, H, d = q.shape
    rows = S // GRID_W
    kh = min(NA_KH_MAX, rows)

    def grid(t):
        return t.reshape(B, rows, GRID_W, H, d).transpose(0, 3, 1, 2, 4)

    qg = grid(q * (d ** -0.5))
    kg, vg = grid(k), grid(v)
    col = jnp.arange(GRID_W)
    cs = jnp.clip(col - NA_KW // 2, 0, GRID_W - NA_KW)
    col_mask = (col[None, :] >= cs[:, None]) & (col[None, :] < cs[:, None] + NA_KW)
    dc_idx = jnp.clip(col[None, :] - col[:, None], -(NA_KW - 1), NA_KW - 1) + (NA_KW - 1)
    rpb_f = rpb.astype(jnp.float32)

    def row_block(r):
        rs = jnp.clip(r - kh // 2, 0, rows - kh)
        qr = lax.dynamic_index_in_dim(qg, r, axis=2, keepdims=False)
        kb = lax.dynamic_slice_in_dim(kg, rs, kh, axis=2)
        vb = lax.dynamic_slice_in_dim(vg, rs, kh, axis=2)
        dr_idx = rs + jnp.arange(kh) - r + (NA_KH_MAX - 1)
        bias = rpb_f[:, dr_idx[None, :, None], dc_idx[:, None, :]]
        s = jnp.einsum('bhqd,bhiwd->bhqiw', qr, kb).astype(jnp.float32) + bias[None]
        s = jnp.where(col_mask[:, None, :], s, -1e30)
        prob = jax.nn.softmax(s.reshape(B, H, GRID_W, kh * GRID_W), axis=-1)
        prob = prob.reshape(B, H, GRID_W, kh, GRID_W).astype(v.dtype)
        return jnp.einsum('bhqiw,bhiwd->bhqd', prob, vb)

    out = lax.map(row_block, jnp.arange(rows))
    return out.transpose(1, 0, 3, 2, 4).reshape(B, S, H * d)


def retention_direction(q, k, v, log_gamma, include_diag):
    B, H, S, dk = q.shape
    dv = v.shape[-1]
    C = RET_CHUNK
    n = S // C
    i = jnp.arange(C, dtype=jnp.float32)
    diff = i[:, None] - i[None, :]
    mask = (diff >= 0) if include_diag else (diff > 0)
    lg = log_gamma[:, None, None]
    d_intra = jnp.where(mask[None], jnp.exp(jnp.maximum(diff, 0.0)[None] * lg), 0.0).astype(q.dtype)
    xi = jnp.exp((i + 1.0)[None, :] * log_gamma[:, None]).astype(q.dtype)[:, :, None]
    zeta = jnp.exp((C - 1.0 - i)[None, :] * log_gamma[:, None]).astype(q.dtype)[:, :, None]
    chunk_decay = jnp.exp(C * log_gamma).astype(q.dtype)[:, None, None]

    def chunks(t):
        return t.reshape(B, H, n, C, t.shape[-1]).transpose(2, 0, 1, 3, 4)

    def step(state, inp):
        qc, kc, vc = inp
        intra = jnp.einsum('bhij,bhjv->bhiv', jnp.einsum('bhid,bhjd->bhij', qc, kc) * d_intra, vc)
        cross = jnp.einsum('bhid,bhdv->bhiv', qc, state) * xi
        state = state * chunk_decay + jnp.einsum('bhjd,bhjv->bhdv', kc * zeta, vc)
        return state, intra + cross

    state0 = jnp.zeros((B, H, dk, dv), q.dtype)
    _, out = lax.scan(step, state0, (chunks(q), chunks(k), chunks(v)))
    return out.transpose(1, 2, 0, 3, 4).reshape(B, H, S, dv)


def token_mixer(h, w_in, rpb, dec_f, dec_b, gn_g, w_out):
    B, S, _ = h.shape
    proj = h @ w_in
    split_at = np.cumsum(np.array(IN_SPLITS))[:-1].tolist()
    na_q, na_k, na_v, r_q, r_k, r_v, r_g, gate_na, gate_ret = jnp.split(proj, split_at, axis=-1)
    na = neighbourhood_attention(na_q.reshape(B, S, NA_HEADS, NA_HEAD_DIM),
                                 na_k.reshape(B, S, NA_HEADS, NA_HEAD_DIM),
                                 na_v.reshape(B, S, NA_HEADS, NA_HEAD_DIM), rpb)
    pos = jnp.arange(S)
    rq = rotary(r_q.reshape(B, S, RET_HEADS, RET_QK_DIM), pos).transpose(0, 2, 1, 3)
    rk = (rotary(r_k.reshape(B, S, RET_HEADS, RET_QK_DIM), pos) * (RET_QK_DIM ** -0.5)).transpose(0, 2, 1, 3)
    rv = r_v.reshape(B, S, RET_HEADS, RET_V_DIM).transpose(0, 2, 1, 3)
    lg_f = jax.nn.log_sigmoid(dec_f.astype(jnp.float32))
    lg_b = jax.nn.log_sigmoid(dec_b.astype(jnp.float32))
    fwd = retention_direction(rq, rk, rv, lg_f, True)
    bwd = retention_direction(rq[:, :, ::-1], rk[:, :, ::-1], rv[:, :, ::-1], lg_b, False)[:, :, ::-1]
    rf = (fwd + bwd).astype(jnp.float32)
    mu = jnp.mean(rf, axis=-1, keepdims=True)
    var = jnp.mean(jnp.square(rf - mu), axis=-1, keepdims=True)
    rn = ((rf - mu) * lax.rsqrt(var + EPS)).transpose(0, 2, 1, 3).reshape(B, S, RET_V_WIDTH)
    ret = (rn * gn_g.astype(jnp.float32)).astype(h.dtype) * jax.nn.silu(r_g)
    mix = jax.nn.sigmoid(gate_na) * na + jax.nn.sigmoid(gate_ret) * ret
    return mix @ w_out


def hierarchical_moe(h, w_rg, w_re, w_g, w_u, w_d):
    B, S, D = h.shape
    T = B * S
    ht = h.reshape(T, D)
    g_prob = jax.nn.softmax((ht @ w_rg).astype(jnp.float32), axis=-1)
    p_g, g_sel = lax.top_k(g_prob, 1)
    p_g, g_sel = p_g[:, 0], g_sel[:, 0]
    e_logits = (ht @ w_re).astype(jnp.float32).reshape(T, N_GROUPS, EXPERTS_PER_GROUP)
    e_sel = jnp.take_along_axis(e_logits, g_sel[:, None, None], axis=1)[:, 0]
    top_v, top_i = lax.top_k(e_sel, TOP_K)
    weights = p_g[:, None] * jax.nn.softmax(top_v, axis=-1)
    eid = (g_sel[:, None] * EXPERTS_PER_GROUP + top_i).reshape(-1).astype(jnp.int32)
    tok = jnp.repeat(jnp.arange(T, dtype=jnp.int32), TOP_K)
    wts = weights.reshape(-1)
    A = T * TOP_K
    order = jnp.argsort(eid)
    se, stok, sw = eid[order], tok[order], wts[order]
    counts = jax.ops.segment_sum(jnp.ones_like(eid), eid, num_segments=N_EXPERTS)
    starts = jnp.cumsum(counts) - counts
    blocks_per = (counts + MOE_BLOCK - 1) // MOE_BLOCK
    blk_end = jnp.cumsum(blocks_per)
    blk_start = blk_end - blocks_per
    dest = blk_start[se] * MOE_BLOCK + (jnp.arange(A, dtype=jnp.int32) - starts[se])
    nb = -(-A // MOE_BLOCK) + N_EXPERTS
    xp = jnp.zeros((nb * MOE_BLOCK, D), h.dtype).at[dest].set(ht[stok])
    block_expert = jnp.clip(jnp.searchsorted(blk_end, jnp.arange(nb), side='right'), 0, N_EXPERTS - 1)

    def expert_block(args):
        xb, e = args
        return (jax.nn.silu(xb @ w_g[e]) * (xb @ w_u[e])) @ w_d[e]

    yp = lax.map(expert_block, (xp.reshape(nb, MOE_BLOCK, D), block_expert)).reshape(nb * MOE_BLOCK, D)
    y = jnp.zeros((T, D), h.dtype).at[stok].add(yp[dest] * sw[:, None].astype(h.dtype))
    return y.reshape(B, S, D)


def setup_inputs(seed: int = 0) -> dict:
    key = jax.random.key(seed)
    ks = jax.random.split(key, 20)

    def nrm(k, shape, scale):
        return jax.random.normal(k, shape, jnp.float32) * scale

    base_decay = jnp.log(2.0 ** (5.0 + jnp.arange(RET_HEADS, dtype=jnp.float32)) - 1.0)
    return {
        "x": nrm(ks[0], (BATCH, SEQ, D_MODEL), 1.0),
        "p": nrm(ks[1], (DEPTH, BATCH, SEQ, PLE_DIM), 1.0),
        "norm_mix": 1.0 + nrm(ks[2], (DEPTH, D_MODEL), 0.05),
        "w_in": nrm(ks[3], (DEPTH, D_MODEL, N_IN), D_MODEL ** -0.5),
        "na_rpb": nrm(ks[4], (DEPTH, NA_HEADS, 2 * NA_KH_MAX - 1, 2 * NA_KW - 1), 0.1),
        "ret_decay_fwd": base_decay[None, :] + nrm(ks[5], (DEPTH, RET_HEADS), 0.1),
        "ret_decay_bwd": base_decay[None, :] + nrm(ks[6], (DEPTH, RET_HEADS), 0.1),
        "ret_gn": 1.0 + nrm(ks[7], (DEPTH, RET_V_WIDTH), 0.05),
        "w_out": nrm(ks[8], (DEPTH, D_MODEL, D_MODEL), D_MODEL ** -0.5),
        "norm_ffn": 1.0 + nrm(ks[9], (DEPTH, D_MODEL), 0.05),
        "w_router_group": nrm(ks[10], (DEPTH, D_MODEL, N_GROUPS), D_MODEL ** -0.5),
        "w_router_expert": nrm(ks[11], (DEPTH, D_MODEL, N_EXPERTS), D_MODEL ** -0.5),
        "w_gate_e": nrm(ks[12], (DEPTH, N_EXPERTS, D_MODEL, D_FF_EXPERT), D_MODEL ** -0.5),
        "w_up_e": nrm(ks[13], (DEPTH, N_EXPERTS, D_MODEL, D_FF_EXPERT), D_MODEL ** -0.5),
        "w_down_e": nrm(ks[14], (DEPTH, N_EXPERTS, D_FF_EXPERT, D_MODEL), D_FF_EXPERT ** -0.5),
        "norm_ple": 1.0 + nrm(ks[15], (DEPTH, D_MODEL), 0.05),
        "w_ple_gate": nrm(ks[16], (DEPTH, D_MODEL, D_MODEL), D_MODEL ** -0.5),
        "w_ple_proj": nrm(ks[17], (DEPTH, PLE_DIM, D_MODEL), PLE_DIM ** -0.5),
        "norm_final": 1.0 + nrm(ks[18], (D_MODEL,), 0.05),
    }


def reference(x, p, norm_mix, w_in, na_rpb, ret_decay_fwd, ret_decay_bwd, ret_gn, w_out,
              norm_ffn, w_router_group, w_router_expert, w_gate_e, w_up_e, w_down_e,
              norm_ple, w_ple_gate, w_ple_proj, norm_final):
    for i in range(DEPTH):
        h = rmsnorm(x, norm_mix[i])
        x = x + token_mixer(h, w_in[i], na_rpb[i], ret_decay_fwd[i], ret_decay_bwd[i], ret_gn[i], w_out[i])
        h = rmsnorm(x, norm_ffn[i])
        x = x + hierarchical_moe(h, w_router_group[i], w_router_expert[i], w_gate_e[i], w_up_e[i], w_down_e[i])
        gate = jax.nn.sigmoid(rmsnorm(x, norm_ple[i]) @ w_ple_gate[i])
        x = x + gate * (p[i] @ w_ple_proj[i])
    return rmsnorm(x, norm_final)
```

```python
import functools

import jax
import jax.numpy as jnp
from jax import lax
from jax.experimental import pallas as pl
from jax.experimental.pallas import tpu as pltpu

F32 = jnp.float32
BF16 = jnp.bfloat16

LANE = 128
EPS = 1e-6
GRID_W = 64
NA_HEAD_DIM = 128
NA_KH = 8
NA_KW = 16
NA_GROUP_ROWS = 4
NA_WIN_ROWS = NA_GROUP_ROWS + NA_KH
RET_HEADS = 8
RET_CHUNK = 256
ROPE_BASE = 10000.0
N_GROUPS = 4
EXPERTS_PER_GROUP = 8
N_EXPERTS = N_GROUPS * EXPERTS_PER_GROUP
TOP_K = 2
MOE_BLOCK = 256
ROUTER_PAD = 128
VMEM_LIMIT = 56 * 1024 * 1024


def _cparams(sem):
    return pltpu.CompilerParams(dimension_semantics=sem, vmem_limit_bytes=VMEM_LIMIT)


def _rms(x, g):
    ms = jnp.mean(x * x, axis=-1, keepdims=True)
    return x * lax.rsqrt(ms + EPS) * g


def _cat(ref, sl=None):
    n = ref.shape[0]
    if sl is None:
        return jnp.concatenate([ref[c] for c in range(n)], axis=-1)
    return jnp.concatenate([ref[c, sl, :] for c in range(n)], axis=-1)


def _inproj_kernel(x_ref, g_ref, w_ref, cs_ref, cos_ref, sin_ref, o_ref, h_ref, *, rot_lo, rot_hi):
    j = pl.program_id(1)

    @pl.when(j == 0)
    def _():
        h_ref[...] = _rms(x_ref[...], g_ref[...]).astype(BF16)

    acc = jnp.dot(h_ref[...], w_ref[...], preferred_element_type=F32) * cs_ref[...]
    nchunk = o_ref.shape[0]
    is_rot = jnp.logical_and(j >= rot_lo, j < rot_hi)

    @pl.when(jnp.logical_not(is_rot))
    def _():
        for c in range(nchunk):
            o_ref[c] = acc[:, c * LANE:(c + 1) * LANE].astype(BF16)

    @pl.when(is_rot)
    def _():
        cos = cos_ref[...]
        sin = sin_ref[...]
        for hh in range(nchunk // 2):
            x1 = acc[:, (2 * hh) * LANE:(2 * hh + 1) * LANE]
            x2 = acc[:, (2 * hh + 1) * LANE:(2 * hh + 2) * LANE]
            o_ref[2 * hh] = (x1 * cos - x2 * sin).astype(BF16)
            o_ref[2 * hh + 1] = (x1 * sin + x2 * cos).astype(BF16)


def _inproj(x, g, w, colscale, cos, sin, seq, rot_lo_col, rot_hi_col):
    T, D = x.shape
    N = w.shape[1]
    tm = min(1024, seq)
    tn = 1024
    nck = tn // LANE
    sblk = seq // tm
    kern = functools.partial(_inproj_kernel, rot_lo=rot_lo_col // tn, rot_hi=rot_hi_col // tn)
    return pl.pallas_call(
        kern,
        out_shape=jax.ShapeDtypeStruct((N // LANE, T, LANE), BF16),
        grid=(T // tm, N // tn),
        in_specs=[
            pl.BlockSpec((tm, D), lambda i, j: (i, 0)),
            pl.BlockSpec((1, D), lambda i, j: (0, 0)),
            pl.BlockSpec((D, tn), lambda i, j: (0, j)),
            pl.BlockSpec((1, tn), lambda i, j: (0, j)),
            pl.BlockSpec((tm, LANE), lambda i, j: (i % sblk, 0)),
            pl.BlockSpec((tm, LANE), lambda i, j: (i % sblk, 0)),
        ],
        out_specs=pl.BlockSpec((nck, tm, LANE), lambda i, j: (j, i, 0)),
        scratch_shapes=[pltpu.VMEM((tm, D), BF16)],
        compiler_params=_cparams(("arbitrary", "arbitrary")),
        name="inproj",
    )(x, g, w, colscale, cos, sin)


def _na_table(rpb, rows):
    G, Wn = NA_GROUP_ROWS, NA_WIN_ROWS
    r0 = jnp.array([0, G, rows - G], jnp.int32)
    ws = jnp.clip(r0 - NA_KH // 2, 0, rows - Wn)
    r = r0[:, None] + jnp.arange(G)[None, :]
    rs = jnp.clip(r - NA_KH // 2, 0, rows - NA_KH)
    kr = ws[:, None] + jnp.arange(Wn)[None, :]
    row_ok = (kr[:, None, :] >= rs[:, :, None]) & (kr[:, None, :] < rs[:, :, None] + NA_KH)
    dr = jnp.clip(kr[:, None, :] - r[:, :, None] + (NA_KH - 1), 0, 2 * NA_KH - 2)
    col = jnp.arange(GRID_W)
    cs = jnp.clip(col - NA_KW // 2, 0, GRID_W - NA_KW)
    col_ok = (col[None, :] >= cs[:, None]) & (col[None, :] < cs[:, None] + NA_KW)
    dc = jnp.clip(col[None, :] - col[:, None], -(NA_KW - 1), NA_KW - 1) + (NA_KW - 1)
    bias = rpb.astype(F32)[:, dr[:, :, None, :, None], dc[None, None, :, None, :]]
    ok = row_ok[:, :, None, :, None] & col_ok[None, None, :, None, :]
    tbl = jnp.where(ok[None], bias, -1e30)
    H = rpb.shape[0]
    return tbl.reshape(H, 3, G * GRID_W, Wn * GRID_W)


def _na_kernel(q_ref, k_ref, v_ref, gate_ref, tbl_ref, o_ref, *, n_groups, rows):
    gq = NA_GROUP_ROWS * GRID_W
    gk = NA_WIN_ROWS * GRID_W

    def body(g, carry):
        ws = jnp.clip(NA_GROUP_ROWS * g - NA_KH // 2, 0, rows - NA_WIN_ROWS)
        qoff = pl.multiple_of(g * gq, gq)
        koff = pl.multiple_of(ws * GRID_W, GRID_W)
        q = q_ref[pl.ds(qoff, gq), :]
        kw = k_ref[pl.ds(koff, gk), :]
        vw = v_ref[pl.ds(koff, gk), :]
        var = jnp.where(g == 0, 0, jnp.where(g == n_groups - 1, 2, 1))
        s = lax.dot_general(q, kw, (((1,), (1,)), ((), ())), preferred_element_type=F32) + tbl_ref[var]
        m = jnp.max(s, axis=-1, keepdims=True)
        p = jnp.exp(s - m)
        l = jnp.sum(p, axis=-1, keepdims=True)
        o = jnp.dot(p.astype(BF16), vw, preferred_element_type=F32) / l
        gt = jax.nn.sigmoid(gate_ref[pl.ds(qoff, gq), :].astype(F32))
        o_ref[pl.ds(qoff, gq), :] = (gt * o).astype(BF16)
        return carry

    lax.fori_loop(0, n_groups, body, 0)


def _na(proj, tbl, batch, seq, q_base, k_base, v_base, gate_base):
    H = tbl.shape[0]
    T = proj.shape[1]
    rows = seq // GRID_W
    n_groups = rows // NA_GROUP_ROWS

    def spec(base):
        return pl.BlockSpec((None, seq, LANE), lambda h, b: (base + h, b, 0))

    kern = functools.partial(_na_kernel, n_groups=n_groups, rows=rows)
    return pl.pallas_call(
        kern,
        out_shape=jax.ShapeDtypeStruct((H, T, LANE), BF16),
        grid=(H, batch),
        in_specs=[spec(q_base), spec(k_base), spec(v_base), spec(gate_base),
                  pl.BlockSpec((None,) + tbl.shape[1:], lambda h, b: (h, 0, 0, 0))],
        out_specs=pl.BlockSpec((None, seq, LANE), lambda h, b: (h, b, 0)),
        compiler_params=_cparams(("arbitrary", "arbitrary")),
        name="na_attention",
    )(proj, proj, proj, proj, tbl)


def _ret_tables(dec_f, dec_b, dv):
    C = RET_CHUNK
    lg_f = jax.nn.log_sigmoid(dec_f.astype(F32))
    lg_b = jax.nn.log_sigmoid(dec_b.astype(F32))
    i = jnp.arange(C, dtype=F32)
    diff = i[:, None] - i[None, :]
    d = jnp.where(diff[None] >= 0,
                  jnp.exp(jnp.maximum(diff, 0.0)[None] * lg_f[:, None, None]),
                  jnp.exp(jnp.maximum(-diff, 0.0)[None] * lg_b[:, None, None]))

    def rowtab(e):
        return jnp.broadcast_to(jnp.exp(e)[:, :, None], (e.shape[0], C, dv))

    xi_f = rowtab((i + 1.0)[None, :] * lg_f[:, None])
    xi_b = rowtab((C - i)[None, :] * lg_b[:, None])
    z_f = rowtab((C - 1.0 - i)[None, :] * lg_f[:, None])
    z_b = rowtab(i[None, :] * lg_b[:, None])
    dc_f = jnp.broadcast_to(jnp.exp(C * lg_f)[:, None, None], (lg_f.shape[0], 1, dv))
    dc_b = jnp.broadcast_to(jnp.exp(C * lg_b)[:, None, None], (lg_b.shape[0], 1, dv))
    return d, xi_f, xi_b, z_f, z_b, dc_f, dc_b


def _ret_state_kernel(k_ref, v_ref, zb_ref, dec_ref, sb_ref, st_ref, *, cpb):
    @pl.when(pl.program_id(2) == 0)
    def _():
        st_ref[...] = jnp.zeros_like(st_ref)

    C = RET_CHUNK
    for c in reversed(range(cpb)):
        sl = slice(c * C, (c + 1) * C)
        st = st_ref[...]
        sb_ref[c] = st.astype(BF16)
        k = _cat(k_ref, sl)
        v = _cat(v_ref, sl)
        kz = (k.astype(F32) * zb_ref[...]).astype(BF16)
        st_ref[...] = st * dec_ref[...] + lax.dot_general(
            kz, v, (((0,), (0,)), ((), ())), preferred_element_type=F32)


def _ret_fwd_kernel(q_ref, k_ref, v_ref, rg_ref, gr_ref, sb_ref, d_ref, xif_ref, xib_ref, zf_ref,
                    dec_ref, gn_ref, o_ref, st_ref, *, cpb):
    @pl.when(pl.program_id(2) == 0)
    def _():
        st_ref[...] = jnp.zeros_like(st_ref)

    C = RET_CHUNK
    nout = o_ref.shape[0]
    for c in range(cpb):
        sl = slice(c * C, (c + 1) * C)
        q = _cat(q_ref, sl)
        k = _cat(k_ref, sl)
        v = _cat(v_ref, sl)
        a = lax.dot_general(q, k, (((1,), (1,)), ((), ())), preferred_element_type=F32)
        o = jnp.dot((a * d_ref[...]).astype(BF16), v, preferred_element_type=F32)
        st = st_ref[...]
        o = o + jnp.dot(q, st.astype(BF16), preferred_element_type=F32) * xif_ref[...]
        o = o + jnp.dot(q, sb_ref[c], preferred_element_type=F32) * xib_ref[...]
        kz = (k.astype(F32) * zf_ref[...]).astype(BF16)
        st_ref[...] = st * dec_ref[...] + lax.dot_general(
            kz, v, (((0,), (0,)), ((), ())), preferred_element_type=F32)
        mu = jnp.mean(o, axis=-1, keepdims=True)
        dlt = o - mu
        var = jnp.mean(dlt * dlt, axis=-1, keepdims=True)
        rn = dlt * lax.rsqrt(var + EPS) * gn_ref[...]
        rg = _cat(rg_ref, sl).astype(F32)
        gr = _cat(gr_ref, sl).astype(F32)
        ret = rn * (rg * jax.nn.sigmoid(rg)) * jax.nn.sigmoid(gr)
        for cc in range(nout):
            o_ref[cc, sl, :] = ret[:, cc * LANE:(cc + 1) * LANE].astype(BF16)


def _retention(proj, tabs, gn, batch, seq, q_base, k_base, v_base, rg_base, gr_base):
    d, xi_f, xi_b, z_f, z_b, dc_f, dc_b = tabs
    H, C, dv = xi_f.shape
    T = proj.shape[1]
    hc = dv // LANE
    tb = min(1024, seq)
    cpb = tb // C
    nblk = seq // tb
    nck = seq // C
    gn3 = gn.reshape(H, 1, dv)

    def hspec(base, rev):
        if rev:
            return pl.BlockSpec((hc, tb, LANE), lambda b, h, i: (base // hc + h, b * nblk + nblk - 1 - i, 0))
        return pl.BlockSpec((hc, tb, LANE), lambda b, h, i: (base // hc + h, b * nblk + i, 0))

    def tspec(a):
        return pl.BlockSpec((None,) + a.shape[1:], lambda b, h, i: (h, 0, 0))

    sem = ("arbitrary", "arbitrary", "arbitrary")
    sb = pl.pallas_call(
        functools.partial(_ret_state_kernel, cpb=cpb),
        out_shape=jax.ShapeDtypeStruct((batch, H, nck, dv, dv), BF16),
        grid=(batch, H, nblk),
        in_specs=[hspec(k_base, True), hspec(v_base, True), tspec(z_b), tspec(dc_b)],
        out_specs=pl.BlockSpec((None, None, cpb, dv, dv), lambda b, h, i: (b, h, nblk - 1 - i, 0, 0)),
        scratch_shapes=[pltpu.VMEM((dv, dv), F32)],
        compiler_params=_cparams(sem),
        name="ret_bwd_state",
    )(proj, proj, z_b, dc_b)

    return pl.pallas_call(
        functools.partial(_ret_fwd_kernel, cpb=cpb),
        out_shape=jax.ShapeDtypeStruct((H * hc, T, LANE), BF16),
        grid=(batch, H, nblk),
        in_specs=[hspec(q_base, False), hspec(k_base, False), hspec(v_base, False),
                  hspec(rg_base, False), hspec(gr_base, False),
                  pl.BlockSpec((None, None, cpb, dv, dv), lambda b, h, i: (b, h, i, 0, 0)),
                  tspec(d), tspec(xi_f), tspec(xi_b), tspec(z_f), tspec(dc_f), tspec(gn3)],
        out_specs=pl.BlockSpec((hc, tb, LANE), lambda b, h, i: (h, b * nblk + i, 0)),
        scratch_shapes=[pltpu.VMEM((dv, dv), F32)],
        compiler_params=_cparams(sem),
        name="ret_fwd",
    )(proj, proj, proj, proj, proj, sb, d, xi_f, xi_b, z_f, dc_f, gn3)


def _outproj_kernel(na_ref, ret_ref, x_ref, w_ref, g_ref, wr_ref, x1_ref, h2_ref, lg_ref):
    nch = na_ref.shape[0]
    tm = x_ref.shape[0]
    mix = jnp.concatenate([na_ref[c] + ret_ref[c] for c in range(nch)], axis=-1)
    x1 = x_ref[...] + jnp.dot(mix, w_ref[...], preferred_element_type=F32)
    x1_ref[...] = x1
    h2 = _rms(x1, g_ref[...])
    for c in range(nch):
        h2_ref[pl.ds(c, tm, stride=nch), :] = h2[:, c * LANE:(c + 1) * LANE]
    lg_ref[...] = jnp.dot(h2.astype(BF16), wr_ref[...], preferred_element_type=F32)


def _outproj(na_g, ret_g, x, w, g, wr):
    T, D = x.shape
    nch = D // LANE
    tm = 256
    act = pl.BlockSpec((nch, tm, LANE), lambda i: (0, i, 0))
    return pl.pallas_call(
        _outproj_kernel,
        out_shape=(jax.ShapeDtypeStruct((T, D), F32),
                   jax.ShapeDtypeStruct((T * nch, LANE), F32),
                   jax.ShapeDtypeStruct((T, ROUTER_PAD), F32)),
        grid=(T // tm,),
        in_specs=[act, act,
                  pl.BlockSpec((tm, D), lambda i: (i, 0)),
                  pl.BlockSpec((D, D), lambda i: (0, 0)),
                  pl.BlockSpec((1, D), lambda i: (0, 0)),
                  pl.BlockSpec((D, ROUTER_PAD), lambda i: (0, 0))],
        out_specs=(pl.BlockSpec((tm, D), lambda i: (i, 0)),
                   pl.BlockSpec((tm * nch, LANE), lambda i: (i, 0)),
                   pl.BlockSpec((tm, ROUTER_PAD), lambda i: (i, 0))),
        compiler_params=_cparams(("arbitrary",)),
        name="outproj_router",
    )(na_g, ret_g, x, w, g, wr)


def _route(logits, nb):
    T = logits.shape[0]
    gl = logits[:, :N_GROUPS]
    el = logits[:, N_GROUPS:N_GROUPS + N_EXPERTS].reshape(T, N_GROUPS, EXPERTS_PER_GROUP)
    g_prob = jax.nn.softmax(gl, axis=-1)
    p_g, g_sel = lax.top_k(g_prob, 1)
    p_g, g_sel = p_g[:, 0], g_sel[:, 0]
    e_sel = jnp.take_along_axis(el, g_sel[:, None, None], axis=1)[:, 0]
    top_v, top_i = lax.top_k(e_sel, TOP_K)
    wts = (p_g[:, None] * jax.nn.softmax(top_v, axis=-1)).reshape(-1)
    eid = (g_sel[:, None] * EXPERTS_PER_GROUP + top_i).reshape(-1).astype(jnp.int32)
    A = T * TOP_K
    onehot = (eid[:, None] == jnp.arange(N_EXPERTS, dtype=jnp.int32)[None, :]).astype(jnp.int32)
    csum = jnp.cumsum(onehot, axis=0)
    rank = jnp.take_along_axis(csum, eid[:, None], axis=1)[:, 0] - 1
    counts = csum[-1]
    blocks_per = (counts + MOE_BLOCK - 1) // MOE_BLOCK
    blk_end = jnp.cumsum(blocks_per)
    blk_start = blk_end - blocks_per
    slot = (blk_start[eid] * MOE_BLOCK + rank).astype(jnp.int32)
    nslots = nb * MOE_BLOCK
    tok = jnp.arange(A, dtype=jnp.int32) // TOP_K
    src_tok = jnp.zeros((nslots,), jnp.int32).at[slot].set(tok)
    sw = jnp.zeros((nslots, 1), F32).at[slot, 0].set(wts)
    block_expert = jnp.clip(jnp.searchsorted(blk_end, jnp.arange(nb), side='right'), 0, N_EXPERTS - 1)
    meta = jnp.concatenate([block_expert.astype(jnp.int32), blk_end[-1:].astype(jnp.int32)])
    return meta, src_tok, sw, slot


def _moe_kernel(meta_ref, src_ref, h2_hbm, wg_ref, wu_ref, wd_ref, sw_ref, yp_ref, xbuf, sem, *, nb, nch):
    i = pl.program_id(0)
    tm = MOE_BLOCK
    nvalid = meta_ref[nb]

    def row_copy(tok, r, slot):
        return pltpu.make_async_copy(
            h2_hbm.at[pl.ds(pl.multiple_of(tok * nch, nch), nch), :],
            xbuf.at[slot, pl.ds(pl.multiple_of(r * nch, nch), nch), :],
            sem.at[slot])

    def issue(blk, slot):
        def body(r, carry):
            row_copy(src_ref[blk * tm + r], r, slot).start()
            return carry
        lax.fori_loop(0, tm, body, 0)

    @pl.when(i == 0)
    def _():
        issue(0, 0)

    @pl.when(i + 1 < nvalid)
    def _():
        issue(i + 1, (i + 1) % 2)

    @pl.when(i < nvalid)
    def _():
        slot = i % 2
        pltpu.make_async_copy(xbuf.at[slot], xbuf.at[slot], sem.at[slot]).wait()
        x = jnp.concatenate(
            [xbuf[slot, pl.ds(c, tm, stride=nch), :].astype(BF16) for c in range(nch)], axis=-1)
        g = jnp.dot(x, wg_ref[...], preferred_element_type=F32)
        u = jnp.dot(x, wu_ref[...], preferred_element_type=F32)
        hmid = (g * jax.nn.sigmoid(g) * u).astype(BF16)
        y = jnp.dot(hmid, wd_ref[...], preferred_element_type=F32) * sw_ref[...]
        for c in range(nch):
            yp_ref[pl.ds(c, tm, stride=nch), :] = y[:, c * LANE:(c + 1) * LANE]

    @pl.when(i >= nvalid)
    def _():
        yp_ref[...] = jnp.zeros_like(yp_ref)


def _moe_ffn(meta, src_tok, sw, h2t, wg, wu, wd, nb):
    E, D, F = wg.shape
    nch = D // LANE
    tm = MOE_BLOCK
    kern = functools.partial(_moe_kernel, nb=nb, nch=nch)
    return pl.pallas_call(
        kern,
        out_shape=jax.ShapeDtypeStruct((nb * tm * nch, LANE), F32),
        grid_spec=pltpu.PrefetchScalarGridSpec(
            num_scalar_prefetch=2,
            grid=(nb,),
            in_specs=[pl.BlockSpec(memory_space=pl.ANY),
                      pl.BlockSpec((None, D, F), lambda i, m, s: (m[i], 0, 0)),
                      pl.BlockSpec((None, D, F), lambda i, m, s: (m[i], 0, 0)),
                      pl.BlockSpec((None, F, D), lambda i, m, s: (m[i], 0, 0)),
                      pl.BlockSpec((tm, 1), lambda i, m, s: (i, 0))],
            out_specs=pl.BlockSpec((tm * nch, LANE), lambda i, m, s: (i, 0)),
            scratch_shapes=[pltpu.VMEM((2, tm * nch, LANE), F32), pltpu.SemaphoreType.DMA((2,))]),
        compiler_params=_cparams(("arbitrary",)),
        name="moe_ffn",
    )(meta, src_tok, h2t, wg, wu, wd, sw)


def _ple_kernel(pos_ref, x1_ref, yp_hbm, p_ref, g_ref, wpg_ref, wpp_ref, gf_ref, o_ref, ybuf, sem,
                *, nch, final):
    i = pl.program_id(0)
    n = pl.num_programs(0)
    tm = x1_ref.shape[0]

    def row_copy(ps, dst_row, slot):
        return pltpu.make_async_copy(
            yp_hbm.at[pl.ds(pl.multiple_of(ps * nch, nch), nch), :],
            ybuf.at[slot, pl.ds(pl.multiple_of(dst_row * nch, nch), nch), :],
            sem.at[slot])

    def issue(blk, slot):
        def body(r, carry):
            for kk in range(TOP_K):
                row_copy(pos_ref[(blk * tm + r) * TOP_K + kk], kk * tm + r, slot).start()
            return carry
        lax.fori_loop(0, tm, body, 0)

    @pl.when(i == 0)
    def _():
        issue(0, 0)

    @pl.when(i + 1 < n)
    def _():
        issue(i + 1, (i + 1) % 2)

    slot = i % 2
    pltpu.make_async_copy(ybuf.at[slot], ybuf.at[slot], sem.at[slot]).wait()
    x2 = x1_ref[...]
    for kk in range(TOP_K):
        x2 = x2 + jnp.concatenate(
            [ybuf[slot, pl.ds(kk * tm * nch + c, tm, stride=nch), :] for c in range(nch)], axis=-1)
    hn = _rms(x2, g_ref[...]).astype(BF16)
    gate = jax.nn.sigmoid(jnp.dot(hn, wpg_ref[...], preferred_element_type=F32))
    pp = jnp.dot(p_ref[...].astype(BF16), wpp_ref[...], preferred_element_type=F32)
    x3 = x2 + gate * pp
    if final:
        x3 = _rms(x3, gf_ref[...])
    o_ref[...] = x3


def _combine_ple(pos, x1, yp, p, g, wpg, wpp, gf, final):
    T, D = x1.shape
    P = p.shape[1]
    nch = D // LANE
    tm = 256
    kern = functools.partial(_ple_kernel, nch=nch, final=final)
    return pl.pallas_call(
        kern,
        out_shape=jax.ShapeDtypeStruct((T, D), F32),
        grid_spec=pltpu.PrefetchScalarGridSpec(
            num_scalar_prefetch=1,
            grid=(T // tm,),
            in_specs=[pl.BlockSpec((tm, D), lambda i, s: (i, 0)),
                      pl.BlockSpec(memory_space=pl.ANY),
                      pl.BlockSpec((tm, P), lambda i, s: (i, 0)),
                      pl.BlockSpec((1, D), lambda i, s: (0, 0)),
                      pl.BlockSpec((D, D), lambda i, s: (0, 0)),
                      pl.BlockSpec((P, D), lambda i, s: (0, 0)),
                      pl.BlockSpec((1, D), lambda i, s: (0, 0))],
            out_specs=pl.BlockSpec((tm, D), lambda i, s: (i, 0)),
            scratch_shapes=[pltpu.VMEM((2, TOP_K * tm * nch, LANE), F32), pltpu.SemaphoreType.DMA((2,))]),
        compiler_params=_cparams(("arbitrary",)),
        name="combine_ple",
    )(pos, x1, yp, p, g, wpg, wpp, gf)


def kernel(x, p, norm_mix, w_in, na_rpb, ret_decay_fwd, ret_decay_bwd, ret_gn, w_out, norm_ffn,
           w_router_group, w_router_expert, w_gate_e, w_up_e, w_down_e, norm_ple, w_ple_gate,
           w_ple_proj, norm_final):
    B, S, D = x.shape
    depth = w_in.shape[0]
    T = B * S
    rows = S // GRID_W
    na_heads = D // NA_HEAD_DIM
    dk = D // RET_HEADS
    assert D % (RET_HEADS * LANE) == 0 and dk == RET_CHUNK
    assert S % GRID_W == 0 and rows % NA_GROUP_ROWS == 0 and rows >= NA_WIN_ROWS
    assert S % RET_CHUNK == 0 and T % MOE_BLOCK == 0

    nD = D // LANE
    q_na, k_na, v_na, q_r, k_r, v_r, g_r, gate_na, gate_ret = [n * nD for n in range(9)]

    colscale = jnp.ones((9, D), F32)
    colscale = colscale.at[0].set(NA_HEAD_DIM ** -0.5).at[4].set(dk ** -0.5).reshape(1, 9 * D)

    half = dk // 2
    inv = ROPE_BASE ** (-jnp.arange(half, dtype=F32) / half)
    ang = jnp.arange(S).astype(F32)[:, None] * inv[None, :]
    cos, sin = jnp.cos(ang), jnp.sin(ang)

    nb = (T * TOP_K) // MOE_BLOCK + N_EXPERTS
    wr = jnp.concatenate(
        [w_router_group, w_router_expert,
         jnp.zeros((depth, D, ROUTER_PAD - N_GROUPS - N_EXPERTS), w_router_group.dtype)], axis=-1).astype(BF16)

    xt = x.reshape(T, D)
    for i in range(depth):
        proj = _inproj(xt, norm_mix[i].reshape(1, D), w_in[i].astype(BF16), colscale, cos, sin, S,
                       q_r * LANE, v_r * LANE)
        na_g = _na(proj, _na_table(na_rpb[i], rows), B, S, q_na, k_na, v_na, gate_na)
        ret_g = _retention(proj, _ret_tables(ret_decay_fwd[i], ret_decay_bwd[i], dk), ret_gn[i],
                           B, S, q_r, k_r, v_r, g_r, gate_ret)
        x1, h2t, logits = _outproj(na_g, ret_g, xt, w_out[i].astype(BF16), norm_ffn[i].reshape(1, D), wr[i])
        meta, src_tok, sw, slot = _route(logits, nb)
        yp = _moe_ffn(meta, src_tok, sw, h2t, w_gate_e[i].astype(BF16), w_up_e[i].astype(BF16),
                      w_down_e[i].astype(BF16), nb)
        xt = _combine_ple(slot, x1, yp, p[i].reshape(T, -1), norm_ple[i].reshape(1, D),
                          w_ple_gate[i].astype(BF16), w_ple_proj[i].astype(BF16),
                          norm_final.reshape(1, D), final=(i == depth - 1))
    return xt.reshape(B, S, D)
```
